```python
import jax
import jax.numpy as jnp
from jax import lax
import numpy as np

D_MODEL = 1024
BATCH = 4
SEQ = 8192
DEPTH = 4
DEC_BATCH = 16
DEC_SEQ = 4096
PAST_LEN = 128

GRID_W = 64
CHUNK = 128
N_BRANCH = 4
BRANCH_W = D_MODEL // N_BRANCH
HEAD_DIM = 64
A_GROUPS = BRANCH_W // HEAD_DIM
NA_HEADS = BRANCH_W // HEAD_DIM
NA_WIN_R = 8
NA_WIN_C = 16
ML_HEADS = BRANCH_W // HEAD_DIM
N_GATES = 4 * ML_HEADS
CONV_W = 31
D_FF = ((8 * D_MODEL + 3 * 256 - 1) // (3 * 256)) * 256
EPS = 1e-6
IN_SIZES = (BRANCH_W, BRANCH_W,
            BRANCH_W, BRANCH_W, BRANCH_W,
            BRANCH_W, BRANCH_W, BRANCH_W, BRANCH_W, N_GATES,
            BRANCH_W, BRANCH_W,
            N_BRANCH * D_MODEL)
N_IN = 11 * BRANCH_W + N_GATES + N_BRANCH * D_MODEL

kernel_name = "hybrid_gated_branch_encoder"


def _rmsnorm(x, g):
    x32 = x.astype(jnp.float32)
    y = x32 * lax.rsqrt(jnp.mean(x32 * x32, axis=-1, keepdims=True) + EPS)
    return (y * g.astype(jnp.float32)).astype(x.dtype)


def _layernorm(x, g, b):
    x32 = x.astype(jnp.float32)
    mu = jnp.mean(x32, axis=-1, keepdims=True)
    var = jnp.mean(jnp.square(x32 - mu), axis=-1, keepdims=True)
    y = (x32 - mu) * lax.rsqrt(var + EPS)
    return (y * g.astype(jnp.float32) + b.astype(jnp.float32)).astype(x.dtype)


def _spatial_gating(u, v, ln_g, ln_b, w_sp, b_sp):
    bsz, seq, _ = u.shape
    n_chunks = seq // CHUNK
    u = jax.nn.gelu(u)
    v = _layernorm(jax.nn.gelu(v), ln_g, ln_b)
    vg = v.reshape(bsz, n_chunks, CHUNK, A_GROUPS, HEAD_DIM)
    s = jnp.einsum("gts,bnsgc->bntgc", w_sp, vg) + b_sp.T[None, None, :, :, None]
    return u * s.reshape(bsz, seq, BRANCH_W)


def _neighbourhood_attention(q, k, v, rpb):
    bsz, seq, _ = q.shape
    rows = seq // GRID_W
    kr = min(NA_WIN_R, rows)

    def grid(t):
        return t.reshape(bsz, rows, GRID_W, NA_HEADS, HEAD_DIM).transpose(0, 3, 1, 2, 4)

    qg = grid(q) * (HEAD_DIM ** -0.5)
    kg, vg = grid(k), grid(v)
    r_ar = np.arange(rows)
    c_ar = np.arange(GRID_W)
    row_idx = np.clip(r_ar - kr // 2, 0, rows - kr)[:, None] + np.arange(kr)[None, :]
    col_idx = np.clip(c_ar - NA_WIN_C // 2, 0, GRID_W - NA_WIN_C)[:, None] + np.arange(NA_WIN_C)[None, :]
    k_rows = kg[:, :, row_idx]
    v_rows = vg[:, :, row_idx]
    band = jnp.einsum("bhrqd,bhrakd->bhraqk", qg, k_rows)
    q_sel = c_ar[:, None]
    scores = band[..., q_sel, col_idx].astype(jnp.float32)
    dr = row_idx - r_ar[:, None] + (NA_WIN_R - 1)
    dc = col_idx - c_ar[:, None] + (NA_WIN_C - 1)
    bias = rpb[:, dr[:, :, None, None], dc[None, None, :, :]]
    probs = jax.nn.softmax(scores + bias.astype(jnp.float32)[None], axis=(3, 5)).astype(v.dtype)
    p_band = jnp.zeros(band.shape, v.dtype).at[..., q_sel, col_idx].set(probs)
    out = jnp.einsum("bhraqk,bhrakd->bhrqd", p_band, v_rows)
    return out.transpose(0, 2, 3, 1, 4).reshape(bsz, seq, BRANCH_W)


def _mlstm_direction(q, k, v, log_i, log_f):
    bsz, seq, _ = q.shape
    n = seq // CHUNK

    def heads(t):
        return t.reshape(bsz, n, CHUNK, ML_HEADS, HEAD_DIM).transpose(0, 3, 1, 2, 4)

    def gate(t):
        return t.reshape(bsz, n, CHUNK, ML_HEADS).transpose(0, 3, 1, 2)

    qh, kh, vh = heads(q), heads(k) * (HEAD_DIM ** -0.5), heads(v)
    li, lf = gate(log_i), gate(log_f)
    b = jnp.cumsum(lf, axis=-1)
    b_last = b[..., -1]
    a = b_last[..., None] - b + li
    m_loc = jnp.max(a, axis=-1)
    w = jnp.exp(a - m_loc[..., None])
    s_c = jnp.einsum("bhnsv,bhnsk->bhnvk", vh * w[..., None], kh)
    s_n = jnp.einsum("bhns,bhnsk->bhnk", w, kh)

    def step(carry, xs):
        c_st, n_st, m_st = carry
        bl, ml, sc, sn = xs
        m_new = jnp.maximum(bl + m_st, ml)
        f_old = jnp.exp(bl + m_st - m_new)
        f_loc = jnp.exp(ml - m_new)
        c_new = f_old[..., None, None] * c_st + f_loc[..., None, None] * sc
        n_new = f_old[..., None] * n_st + f_loc[..., None] * sn
        return (c_new, n_new, m_new), (c_st, n_st, m_st)

    init = (jnp.zeros((bsz, ML_HEADS, HEAD_DIM, HEAD_DIM), jnp.float32),
            jnp.zeros((bsz, ML_HEADS, HEAD_DIM), jnp.float32),
            jnp.zeros((bsz, ML_HEADS), jnp.float32))
    xs = (jnp.moveaxis(b_last, 2, 0), jnp.moveaxis(m_loc, 2, 0),
          jnp.moveaxis(s_c, 2, 0), jnp.moveaxis(s_n, 2, 0))
    _, (c_prev, n_prev, m_prev) = lax.scan(step, init, xs)
    c_prev = jnp.moveaxis(c_prev, 0, 2)
    n_prev = jnp.moveaxis(n_prev, 0, 2)
    m_prev = jnp.moveaxis(m_prev, 0, 2)

    causal = np.tril(np.ones((CHUNK, CHUNK), dtype=bool))
    d_log = jnp.where(causal, b[..., :, None] - b[..., None, :] + li[..., None, :], -jnp.inf)
    m_inter = b + m_prev[..., None]
    m_t = jnp.maximum(m_inter, jnp.max(d_log, axis=-1))
    d_mat = jnp.exp(d_log - m_t[..., None])
    w_ts = jnp.einsum("bhntd,bhnsd->bhnts", qh, kh) * d_mat
    scale_inter = jnp.exp(m_inter - m_t)
    num = (jnp.einsum("bhnts,bhnsd->bhntd", w_ts, vh)
           + scale_inter[..., None] * jnp.einsum("bhnvk,bhntk->bhntv", c_prev, qh))
    den = jnp.sum(w_ts, axis=-1) + scale_inter * jnp.einsum("bhnk,bhntk->bhnt", n_prev, qh)
    h = num / jnp.maximum(jnp.abs(den), jnp.exp(-m_t))[..., None]
    return h.transpose(0, 2, 3, 1, 4).reshape(bsz, seq, ML_HEADS * HEAD_DIM)


def _mlstm_branch(q, k, v, o, gates, gate_b, hnorm_g):
    dt = q.dtype
    f32 = jnp.float32
    q32, k32, v32 = q.astype(f32), k.astype(f32), v.astype(f32)
    bsz, seq, _ = q.shape
    g = (gates.astype(f32) + gate_b.astype(f32)).reshape(bsz, seq, 4, ML_HEADS)
    li_f, lf_f = g[:, :, 0], jax.nn.log_sigmoid(g[:, :, 1])
    li_b, lf_b = g[:, :, 2], jax.nn.log_sigmoid(g[:, :, 3])
    h_fwd = _mlstm_direction(q32, k32, v32, li_f, lf_f)
    h_bwd = jnp.flip(_mlstm_direction(jnp.flip(q32, 1), jnp.flip(k32, 1), jnp.flip(v32, 1),
                                      jnp.flip(li_b, 1), jnp.flip(lf_b, 1)), 1)
    hh = (h_fwd + h_bwd).reshape(bsz, seq, ML_HEADS, HEAD_DIM)
    hh = hh * lax.rsqrt(jnp.mean(hh * hh, axis=-1, keepdims=True) + EPS)
    h = hh.reshape(bsz, seq, BRANCH_W) * hnorm_g.astype(f32)
    return (jax.nn.sigmoid(o.astype(f32)) * h).astype(dt)


def _conv_module(a, g, conv_w, conv_b, ln_g, ln_b):
    y = a * jax.nn.sigmoid(g)
    y = lax.conv_general_dilated(
        y, conv_w[:, None, :].astype(y.dtype), window_strides=(1,),
        padding=[(CONV_W // 2, CONV_W // 2)],
        dimension_numbers=("NWC", "WIO", "NWC"),
        feature_group_count=BRANCH_W) + conv_b
    return jax.nn.silu(_layernorm(y, ln_g, ln_b))


def _encoder_layer(x, c, w_ada, b_ada, g_mix, g_ffn, w_in, a_ln_g, a_ln_b, a_w_sp, a_b_sp,
                   b_rpb, c_gate_b, c_hnorm_g, d_conv_w, d_conv_b, d_ln_g, d_ln_b,
                   w_branch, w_out, w_ffn_in, w_ffn_out):
    bsz, seq, _ = x.shape
    mod = jax.nn.silu(c) @ w_ada + b_ada
    sh1, sc1, gt1, sh2, sc2, gt2 = [m[:, None, :] for m in jnp.split(mod, 6, axis=-1)]
    h = _rmsnorm(x, g_mix) * (1 + sc1) + sh1
    z = h @ w_in
    (a_u, a_v, b_q, b_k, b_v, c_q, c_k, c_v, c_o, c_g, d_a, d_g, br_gate) = jnp.split(
        z, np.cumsum(IN_SIZES)[:-1].tolist(), axis=-1)
    ys = (_spatial_gating(a_u, a_v, a_ln_g, a_ln_b, a_w_sp, a_b_sp),
          _neighbourhood_attention(b_q, b_k, b_v, b_rpb),
          _mlstm_branch(c_q, c_k, c_v, c_o, c_g, c_gate_b, c_hnorm_g),
          _conv_module(d_a, d_g, d_conv_w, d_conv_b, d_ln_g, d_ln_b))
    gates = jax.nn.sigmoid(br_gate.reshape(bsz, seq, N_BRANCH, D_MODEL))
    merged = gates[:, :, 0] * (ys[0] @ w_branch[0])
    for i in range(1, N_BRANCH):
        merged = merged + gates[:, :, i] * (ys[i] @ w_branch[i])
    x = x + gt1 * (merged @ w_out)
    h = _rmsnorm(x, g_ffn) * (1 + sc2) + sh2
    gp, up = jnp.split(h @ w_ffn_in, 2, axis=-1)
    x = x + gt2 * ((jax.nn.silu(gp) * up) @ w_ffn_out)
    return x


def setup_inputs(seed: int = 0) -> dict:
    key = jax.random.key(seed)
    ks = jax.random.split(key, 32)
    f32 = jnp.float32

    def nrm(k, shape, s):
        return jax.random.normal(k, shape, f32) * s

    L = DEPTH
    i_bias = nrm(ks[10], (L, 2, 1, ML_HEADS), 0.1)
    f_bias = 3.0 + 3.0 * jax.random.uniform(ks[11], (L, 2, 1, ML_HEADS), f32)
    c_gate_b = jnp.concatenate([i_bias, f_bias], axis=2).reshape(L, N_GATES)
    return {
        "x_prompt": nrm(ks[0], (BATCH, SEQ, D_MODEL), 1.0),
        "x_sample": nrm(ks[1], (DEC_BATCH, DEC_SEQ, D_MODEL), 1.0),
        "c_prompt": nrm(ks[2], (BATCH, D_MODEL), 1.0),
        "c_sample": nrm(ks[3], (DEC_BATCH, D_MODEL), 1.0),
        "w_ada": nrm(ks[4], (L, D_MODEL, 6 * D_MODEL), 0.5 * D_MODEL ** -0.5),
        "b_ada": nrm(ks[5], (L, 6 * D_MODEL), 0.01),
        "g_norm_mix": 1.0 + nrm(ks[6], (L, D_MODEL), 0.02),
        "g_norm_ffn": 1.0 + nrm(ks[7], (L, D_MODEL), 0.02),
        "w_in": nrm(ks[8], (L, D_MODEL, N_IN), D_MODEL ** -0.5),
        "a_ln_g": 1.0 + nrm(ks[9], (L, BRANCH_W), 0.02),
        "a_ln_b": nrm(ks[12], (L, BRANCH_W), 0.02),
        "a_w_sp": nrm(ks[13], (L, A_GROUPS, CHUNK, CHUNK), CHUNK ** -0.5),
        "a_b_sp": 1.0 + nrm(ks[14], (L, A_GROUPS, CHUNK), 0.02),
        "b_rpb": nrm(ks[15], (L, NA_HEADS, 2 * NA_WIN_R - 1, 2 * NA_WIN_C - 1), 0.1),
        "c_gate_b": c_gate_b,
        "c_hnorm_g": 1.0 + nrm(ks[16], (L, BRANCH_W), 0.02),
        "d_conv_w": nrm(ks[17], (L, CONV_W, BRANCH_W), CONV_W ** -0.5),
        "d_conv_b": nrm(ks[18], (L, BRANCH_W), 0.02),
        "d_ln_g": 1.0 + nrm(ks[19], (L, BRANCH_W), 0.02),
        "d_ln_b": nrm(ks[20], (L, BRANCH_W), 0.02),
        "w_branch": nrm(ks[21], (L, N_BRANCH, BRANCH_W, D_MODEL), BRANCH_W ** -0.5),
        "w_out": nrm(ks[22], (L, D_MODEL, D_MODEL), D_MODEL ** -0.5),
        "w_ffn_in": nrm(ks[23], (L, D_MODEL, 2 * D_FF), D_MODEL ** -0.5),
        "w_ffn_out": nrm(ks[24], (L, D_FF, D_MODEL), D_FF ** -0.5),
        "g_final": 1.0 + nrm(ks[25], (D_MODEL,), 0.02),
    }


def reference(x_prompt, x_sample, c_prompt, c_sample, w_ada, b_ada, g_norm_mix, g_norm_ffn,
              w_in, a_ln_g, a_ln_b, a_w_sp, a_b_sp, b_rpb, c_gate_b, c_hnorm_g,
              d_conv_w, d_conv_b, d_ln_g, d_ln_b, w_branch, w_out, w_ffn_in, w_ffn_out,
              g_final):
    def trunk(x, c):
        for l in range(DEPTH):
            x = _encoder_layer(x, c, w_ada[l], b_ada[l], g_norm_mix[l], g_norm_ffn[l], w_in[l],
                               a_ln_g[l], a_ln_b[l], a_w_sp[l], a_b_sp[l], b_rpb[l],
                               c_gate_b[l], c_hnorm_g[l], d_conv_w[l], d_conv_b[l],
                               d_ln_g[l], d_ln_b[l], w_branch[l], w_out[l],
                               w_ffn_in[l], w_ffn_out[l])
        return _rmsnorm(x, g_final)

    y_prompt = trunk(x_prompt, c_prompt)
    y_sample = trunk(x_sample, c_sample)
    return (y_prompt, y_sample)
```

```python
import functools

import numpy as np
import jax
import jax.numpy as jnp
from jax import lax
from jax.experimental import pallas as pl
from jax.experimental.pallas import tpu as pltpu

F32 = jnp.float32
BF16 = jnp.bfloat16

D_MODEL = 1024
DEPTH = 4
GRID_W = 64
CHUNK = 128
N_BRANCH = 4
BRANCH_W = D_MODEL // N_BRANCH
HEAD_DIM = 64
N_HEADS = BRANCH_W // HEAD_DIM
NA_WIN_R = 8
NA_WIN_C = 16
N_GATES = 4 * N_HEADS
CONV_W = 31
D_FF = 2816
EPS = 1e-6
N_IN = 11 * BRANCH_W + N_GATES + N_BRANCH * D_MODEL

GATE_PAD = 128
N_MAIN = 11 * BRANCH_W + GATE_PAD
NEG_BIG = -1e30
CONV_HALO = 16

V7X_SCOPED_VMEM_CAP = 60000 * 1024


def _vmem_limit(estimate_bytes):
    return int(min(V7X_SCOPED_VMEM_CAP, estimate_bytes))


def _head_id(shape, axis):
    return lax.shift_right_logical(lax.broadcasted_iota(jnp.int32, shape, axis), 6)


def _mod_norm(x, g, scale, shift):
    ms = jnp.mean(x * x, axis=-1, keepdims=True)
    y = x * lax.rsqrt(ms + EPS) * g
    return y * (1.0 + scale) + shift


def _layer_norm(x, g, b):
    mu = jnp.mean(x, axis=-1, keepdims=True)
    xc = x - mu
    var = jnp.mean(xc * xc, axis=-1, keepdims=True)
    return xc * lax.rsqrt(var + EPS) * g + b


def _dot(a, b):
    return jnp.dot(a, b, preferred_element_type=F32)


def _resident(shape):
    nd = len(shape)
    return pl.BlockSpec(shape, lambda *_: (0,) * nd, pipeline_mode=pl.Buffered(1))


def _ada_kernel(c_ref, w_ref, b_ref, o_ref):
    c = c_ref[...]
    s = (c * jax.nn.sigmoid(c)).astype(BF16)
    o_ref[...] = _dot(s, w_ref[...].astype(BF16)) + b_ref[...]


def _ada_call(c_all, w_ada, b_ada):
    n_seq = c_all.shape[0]
    nb = 1536
    n_out = 6 * D_MODEL
    return pl.pallas_call(
        _ada_kernel,
        grid=(DEPTH, n_out // nb),
        in_specs=[
            pl.BlockSpec((n_seq, D_MODEL), lambda l, j: (0, 0)),
            pl.BlockSpec((None, D_MODEL, nb), lambda l, j: (l, 0, j)),
            pl.BlockSpec((None, 1, nb), lambda l, j: (l, 0, j)),
        ],
        out_specs=pl.BlockSpec((None, n_seq, nb), lambda l, j: (l, 0, j)),
        out_shape=jax.ShapeDtypeStruct((DEPTH, n_seq, n_out), F32),
        compiler_params=pltpu.CompilerParams(
            dimension_semantics=("parallel", "parallel"),
            vmem_limit_bytes=_vmem_limit(32 * 2**20)),
        name="ada_mod",
    )(c_all, w_ada, b_ada.reshape(DEPTH, 1, n_out))


def _dense_in_kernel(x_ref, mod_ref, gmix_ref, w_ref, alng_ref, alnb_ref, wsp_ref, bsp_ref, gb_ref,
                     ya_ref, bq_ref, bk_ref, bv_ref, cq_ref, ck_ref, cv_ref, so_ref, cg_ref, dy_ref,
                     *, tm):
    x = x_ref[...]
    h = _mod_norm(x, gmix_ref[...], mod_ref[1:2, :], mod_ref[0:1, :]).astype(BF16)

    def piece(p, width=BRANCH_W):
        return _dot(h, w_ref[:, p * BRANCH_W:p * BRANCH_W + width])

    gu = jax.nn.gelu(piece(0))
    gv = _layer_norm(jax.nn.gelu(piece(1)), alng_ref[...], alnb_ref[...]).astype(BF16)
    hid = _head_id((CHUNK, BRANCH_W), 1)
    for c in range(tm // CHUNK):
        rows = slice(c * CHUNK, (c + 1) * CHUNK)
        s_all = _dot(wsp_ref[...], gv[rows, :])
        s = jnp.zeros((CHUNK, BRANCH_W), F32)
        for g in range(N_HEADS):
            s = jnp.where(hid == g, s_all[g * CHUNK:(g + 1) * CHUNK, :], s)
        ya_ref[rows, :] = (gu[rows, :] * (s + bsp_ref[...])).astype(BF16)

    bq_ref[...] = piece(2).astype(BF16)
    bk_ref[...] = piece(3).astype(BF16)
    bv_ref[...] = piece(4).astype(BF16)
    cq_ref[...] = piece(5).astype(BF16)
    ck_ref[...] = piece(6).astype(BF16)
    cv_ref[...] = piece(7).astype(BF16)
    so_ref[...] = jax.nn.sigmoid(piece(8))
    g = piece(11, GATE_PAD) + gb_ref[...]
    lane = lax.broadcasted_iota(jnp.int32, g.shape, 1)
    is_forget = (lane < N_GATES) & ((lane & 4) == 4)
    log_sig = jnp.minimum(g, 0.0) - jnp.log1p(jnp.exp(-jnp.abs(g)))
    cg_ref[...] = jnp.where(is_forget, log_sig, g)
    dy_ref[...] = piece(9) * jax.nn.sigmoid(piece(10))


def _dense_in_call(x, mod, gmix, w1, alng, alnb, wsp, bsp, gb, *, tm):
    bsz, seq, _ = x.shape
    tok = lambda width: pl.BlockSpec((None, tm, width), lambda b, i: (b, i, 0))
    out_bf = jax.ShapeDtypeStruct((bsz, seq, BRANCH_W), BF16)
    out_f = jax.ShapeDtypeStruct((bsz, seq, BRANCH_W), F32)
    est = (2 * tm * D_MODEL * 4 + D_MODEL * N_MAIN * 2 + 2 * tm * (7 * BRANCH_W * 2 + 2 * BRANCH_W * 4 + GATE_PAD * 4)
           + 8 * tm * D_MODEL * 4 + 8 * 2**20)
    return pl.pallas_call(
        functools.partial(_dense_in_kernel, tm=tm),
        grid=(bsz, seq // tm),
        in_specs=[
            tok(D_MODEL),
            pl.BlockSpec((None, 8, D_MODEL), lambda b, i: (b, 0, 0)),
            _resident((1, D_MODEL)),
            _resident((D_MODEL, N_MAIN)),
            _resident((1, BRANCH_W)),
            _resident((1, BRANCH_W)),
            _resident((N_HEADS * CHUNK, CHUNK)),
            _resident((CHUNK, BRANCH_W)),
            _resident((1, GATE_PAD)),
        ],
        out_specs=[tok(BRANCH_W)] * 8 + [tok(GATE_PAD), tok(BRANCH_W)],
        out_shape=[out_bf] * 7 + [out_f, jax.ShapeDtypeStruct((bsz, seq, GATE_PAD), F32), out_f],
        compiler_params=pltpu.CompilerParams(
            dimension_semantics=("parallel", "parallel"), vmem_limit_bytes=_vmem_limit(est)),
        name="dense_in",
    )(x, mod, gmix, w1, alng, alnb, wsp, bsp, gb)


def _na_kernel(q_ref, k_ref, v_ref, bias_ref, o_ref, *, rows, rb):
    i = pl.program_id(1)
    win = NA_WIN_R * GRID_W
    hid = _head_id((GRID_W, BRANCH_W), 1)

    def body(j, carry):
        r = i * rb + j
        start = jnp.clip(r - NA_WIN_R // 2, 0, rows - NA_WIN_R)
        delta = r - start
        q = q_ref[pl.ds(pl.multiple_of(j * GRID_W, GRID_W), GRID_W), :].astype(F32)
        qs = jnp.concatenate([jnp.where(hid == h, q, 0.0) for h in range(N_HEADS)], axis=0).astype(BF16)
        koff = pl.multiple_of(start * GRID_W, GRID_W)
        kw = k_ref[pl.ds(koff, win), :]
        vw = v_ref[pl.ds(koff, win), :]
        s = lax.dot_general(qs, kw, (((1,), (1,)), ((), ())), preferred_element_type=F32) + bias_ref[delta]
        m = jnp.max(s, axis=-1, keepdims=True)
        e = jnp.exp(s - m)
        inv = 1.0 / jnp.sum(e, axis=-1, keepdims=True)
        res = _dot(e.astype(BF16), vw) * inv
        o = jnp.zeros((GRID_W, BRANCH_W), F32)
        for h in range(N_HEADS):
            o = jnp.where(hid == h, res[h * GRID_W:(h + 1) * GRID_W, :], o)
        o_ref[pl.ds(pl.multiple_of(j * GRID_W, GRID_W), GRID_W), :] = o.astype(BF16)
        return carry

    lax.fori_loop(0, rb, body, 0)


def _na_call(q, k, v, bias, *, rb):
    bsz, seq, _ = q.shape
    rows = seq // GRID_W
    assert rows >= NA_WIN_R and rows % rb == 0
    tq = rb * GRID_W
    win = NA_WIN_R * GRID_W
    est = 4 * tq * BRANCH_W * 2 + 4 * seq * BRANCH_W * 2 + bias.size * 4 + 16 * 2**20
    return pl.pallas_call(
        functools.partial(_na_kernel, rows=rows, rb=rb),
        grid=(bsz, rows // rb),
        in_specs=[
            pl.BlockSpec((None, tq, BRANCH_W), lambda b, i: (b, i, 0)),
            pl.BlockSpec((None, seq, BRANCH_W), lambda b, i: (b, 0, 0)),
            pl.BlockSpec((None, seq, BRANCH_W), lambda b, i: (b, 0, 0)),
            _resident((NA_WIN_R, N_HEADS * GRID_W, win)),
        ],
        out_specs=pl.BlockSpec((None, tq, BRANCH_W), lambda b, i: (b, i, 0)),
        out_shape=jax.ShapeDtypeStruct((bsz, seq, BRANCH_W), BF16),
        compiler_params=pltpu.CompilerParams(
            dimension_semantics=("parallel", "parallel"), vmem_limit_bytes=_vmem_limit(est)),
        name="nbr_attn",
    )(q, k, v, bias)


def _na_bias_table(rpb):
    delta = np.arange(NA_WIN_R)[:, None, None, None]
    qc = np.arange(GRID_W)[None, :, None, None]
    a = np.arange(NA_WIN_R)[None, None, :, None]
    kc = np.arange(GRID_W)[None, None, None, :]
    c0 = np.clip(qc - NA_WIN_C // 2, 0, GRID_W - NA_WIN_C)
    valid = (kc >= c0) & (kc < c0 + NA_WIN_C)
    dr = np.broadcast_to(a - delta + NA_WIN_R - 1, (NA_WIN_R, GRID_W, NA_WIN_R, GRID_W))
    dc = np.clip(np.broadcast_to(kc - qc + NA_WIN_C - 1, dr.shape), 0, 2 * NA_WIN_C - 2)
    valid = np.broadcast_to(valid, dr.shape) & (dr >= 0) & (dr <= 2 * NA_WIN_R - 2)
    dr = np.clip(dr, 0, 2 * NA_WIN_R - 2)
    tab = rpb.astype(F32)[:, dr, dc]
    tab = jnp.where(valid[None], tab, NEG_BIG)
    tab = tab.transpose(1, 0, 2, 3, 4)
    return tab.reshape(NA_WIN_R, N_HEADS * GRID_W, NA_WIN_R * GRID_W)


def _split3(x):
    hi = x.astype(BF16)
    r1 = x - hi.astype(F32)
    mid = r1.astype(BF16)
    lo = (r1 - mid.astype(F32)).astype(BF16)
    return hi, mid, lo


def _mlstm_direction(q, k, v, gates, tri, causal, last, d, c_ref, n_ref, m_ref):
    lf0 = 8 * d + N_HEADS
    hid = _head_id((CHUNK, BRANCH_W), 1)
    hid_row = _head_id((1, BRANCH_W), 1)

    g_hi, g_mid, g_lo = _split3(gates)
    b_all = _dot(tri, g_hi) + _dot(tri, g_mid) + _dot(tri, g_lo)
    li_on_f = pltpu.roll(gates, N_HEADS, axis=1)
    r_t = (li_on_f - b_all).T
    m_prev = m_ref[...]
    m_inter_all = b_all + m_prev
    b_last = b_all[last:last + 1, :]
    a_all = b_last - b_all + li_on_f
    m_loc = jnp.max(a_all, axis=0, keepdims=True)
    w_all = jnp.exp(a_all - m_loc)
    m_new = jnp.maximum(b_last + m_prev, m_loc)
    f_old = jnp.exp(b_last + m_prev - m_new)
    f_loc = jnp.exp(m_loc - m_new)

    qf = q.astype(F32)
    qs = jnp.concatenate([jnp.where(hid == h, qf, 0.0) for h in range(N_HEADS)], axis=0).astype(BF16)
    s_all = lax.dot_general(qs, k, (((1,), (1,)), ((), ())), preferred_element_type=F32)
    c_prev = c_ref[...]
    n_prev = n_ref[...]
    qc = _dot(q, c_prev.astype(BF16))
    qn = qf * n_prev

    w_rows, scale_inter, inv_den = [], [], []
    for h in range(N_HEADS):
        col = lf0 + h
        d_log = jnp.where(causal, b_all[:, col:col + 1] + r_t[col:col + 1, :], NEG_BIG)
        m_inter = m_inter_all[:, col:col + 1]
        m_t = jnp.maximum(m_inter, jnp.max(d_log, axis=-1, keepdims=True))
        w_h = s_all[h * CHUNK:(h + 1) * CHUNK, :] * jnp.exp(d_log - m_t)
        sc = jnp.exp(m_inter - m_t)
        den = (jnp.sum(w_h, axis=-1, keepdims=True)
               + sc * jnp.sum(jnp.where(hid == h, qn, 0.0), axis=-1, keepdims=True))
        w_rows.append(w_h)
        scale_inter.append(sc)
        inv_den.append(1.0 / jnp.maximum(jnp.abs(den), jnp.exp(-m_t)))
    res = _dot(jnp.concatenate(w_rows, axis=0).astype(BF16), v)
    out = jnp.zeros((CHUNK, BRANCH_W), F32)
    for h in range(N_HEADS):
        out = jnp.where(hid == h, (res[h * CHUNK:(h + 1) * CHUNK, :] + scale_inter[h] * qc) * inv_den[h], out)

    w_exp = jnp.zeros((CHUNK, BRANCH_W), F32)
    f_old_row = jnp.zeros((1, BRANCH_W), F32)
    f_loc_row = jnp.zeros((1, BRANCH_W), F32)
    for h in range(N_HEADS):
        col = lf0 + h
        w_exp = jnp.where(hid == h, w_all[:, col:col + 1], w_exp)
        f_old_row = jnp.where(hid_row == h, f_old[:, col:col + 1], f_old_row)
        f_loc_row = jnp.where(hid_row == h, f_loc[:, col:col + 1], f_loc_row)
    vw = (v.astype(F32) * w_exp).astype(BF16)
    s_c = lax.dot_general(k, vw, (((0,), (0,)), ((), ())), preferred_element_type=F32)
    blk = _head_id((BRANCH_W, BRANCH_W), 0) == _head_id((BRANCH_W, BRANCH_W), 1)
    s_n = jnp.sum(k.astype(F32) * w_exp, axis=0, keepdims=True)
    c_ref[...] = f_old_row * c_prev + f_loc_row * jnp.where(blk, s_c, 0.0)
    n_ref[...] = f_old_row * n_prev + f_loc_row * s_n
    m_ref[...] = m_new
    return out


def _mlstm_kernel(qf_ref, kf_ref, vf_ref, gf_ref, qb_ref, kb_ref, vb_ref, gb_ref,
                  hf_ref, hb_ref, cf_ref, nf_ref, mf_ref, cb_ref, nb_ref, mb_ref):
    @pl.when(pl.program_id(1) == 0)
    def _():
        for ref in (cf_ref, nf_ref, mf_ref, cb_ref, nb_ref, mb_ref):
            ref[...] = jnp.zeros(ref.shape, F32)

    t = lax.broadcasted_iota(jnp.int32, (CHUNK, CHUNK), 0)
    s = lax.broadcasted_iota(jnp.int32, (CHUNK, CHUNK), 1)
    lower = s <= t
    upper = s >= t
    hf_ref[...] = _mlstm_direction(qf_ref[...], kf_ref[...], vf_ref[...], gf_ref[...],
                                   lower.astype(BF16), lower, CHUNK - 1, 0, cf_ref, nf_ref, mf_ref)
    hb_ref[...] = _mlstm_direction(qb_ref[...], kb_ref[...], vb_ref[...], gb_ref[...],
                                   upper.astype(BF16), upper, 0, 1, cb_ref, nb_ref, mb_ref)


def _mlstm_call(q, k, v, gates):
    bsz, seq, _ = q.shape
    n = seq // CHUNK
    fwd = lambda width: pl.BlockSpec((None, CHUNK, width), lambda b, i: (b, i, 0))
    bwd = lambda width: pl.BlockSpec((None, CHUNK, width), lambda b, i: (b, n - 1 - i, 0))
    out = jax.ShapeDtypeStruct((bsz, seq, BRANCH_W), F32)
    state = [pltpu.VMEM((BRANCH_W, BRANCH_W), F32), pltpu.VMEM((1, BRANCH_W), F32), pltpu.VMEM((1, GATE_PAD), F32)]
    return pl.pallas_call(
        _mlstm_kernel,
        grid=(bsz, n),
        in_specs=[fwd(BRANCH_W)] * 3 + [fwd(GATE_PAD)] + [bwd(BRANCH_W)] * 3 + [bwd(GATE_PAD)],
        out_specs=[fwd(BRANCH_W), bwd(BRANCH_W)],
        out_shape=[out, out],
        scratch_shapes=state + state,
        compiler_params=pltpu.CompilerParams(
            dimension_semantics=("arbitrary", "arbitrary"), vmem_limit_bytes=_vmem_limit(32 * 2**20)),
        name="mlstm",
    )(q, k, v, gates, q, k, v, gates)


def _conv_kernel(yc_ref, yp_ref, yn_ref, w_ref, cb_ref, g_ref, b_ref, o_ref, buf_ref, *, td, sub):
    i = pl.program_id(1)
    nt = pl.num_programs(1)
    buf_ref[0:CONV_HALO, :] = jnp.where(i > 0, yp_ref[...], 0.0)
    buf_ref[CONV_HALO:CONV_HALO + td, :] = yc_ref[...]
    buf_ref[CONV_HALO + td:2 * CONV_HALO + td, :] = jnp.where(i < nt - 1, yn_ref[...], 0.0)
    first = CONV_HALO - CONV_W // 2

    for sblk in range(td // sub):
        base = sblk * sub
        acc = jnp.zeros((sub, BRANCH_W), F32)
        for j in range(CONV_W):
            acc = acc + w_ref[j:j + 1, :] * buf_ref[base + first + j:base + first + j + sub, :]
        y = _layer_norm(acc + cb_ref[...], g_ref[...], b_ref[...])
        o_ref[base:base + sub, :] = (y * jax.nn.sigmoid(y)).astype(BF16)


def _conv_call(y, w, cb, g, b, *, td, sub=64):
    bsz, seq, _ = y.shape
    hb = td // CONV_HALO
    n_halo = seq // CONV_HALO
    return pl.pallas_call(
        functools.partial(_conv_kernel, td=td, sub=sub),
        grid=(bsz, seq // td),
        in_specs=[
            pl.BlockSpec((None, td, BRANCH_W), lambda b_, i: (b_, i, 0)),
            pl.BlockSpec((None, CONV_HALO, BRANCH_W), lambda b_, i: (b_, jnp.maximum(i * hb - 1, 0), 0)),
            pl.BlockSpec((None, CONV_HALO, BRANCH_W), lambda b_, i: (b_, jnp.minimum((i + 1) * hb, n_halo - 1), 0)),
            _resident((32, BRANCH_W)),
            _resident((1, BRANCH_W)),
            _resident((1, BRANCH_W)),
            _resident((1, BRANCH_W)),
        ],
        out_specs=pl.BlockSpec((None, td, BRANCH_W), lambda b_, i: (b_, i, 0)),
        out_shape=jax.ShapeDtypeStruct((bsz, seq, BRANCH_W), BF16),
        scratch_shapes=[pltpu.VMEM((td + 2 * CONV_HALO, BRANCH_W), F32)],
        compiler_params=pltpu.CompilerParams(
            dimension_semantics=("parallel", "parallel"), vmem_limit_bytes=_vmem_limit(32 * 2**20)),
        name="conv_module",
    )(y, y, y, w, cb, g, b)


def _merge_kernel(x_ref, mod_ref, gmix_ref, wg_ref, ya_ref, yb_ref, hf_ref, hb_ref, so_ref, hng_ref, yd_ref,
                  p_ref, wo_ref, o_ref):
    x = x_ref[...]
    h = _mod_norm(x, gmix_ref[...], mod_ref[1:2, :], mod_ref[0:1, :]).astype(BF16)

    hh = hf_ref[...] + hb_ref[...]
    hid = _head_id(hh.shape, 1)
    sq = hh * hh
    rinv = jnp.zeros(hh.shape, F32)
    for hd in range(N_HEADS):
        ms = jnp.sum(jnp.where(hid == hd, sq, 0.0), axis=-1, keepdims=True) * (1.0 / HEAD_DIM)
        rinv = jnp.where(hid == hd, lax.rsqrt(ms + EPS), rinv)
    yc = (so_ref[...] * (hh * rinv * hng_ref[...])).astype(BF16)

    ys = (ya_ref[...], yb_ref[...], yc, yd_ref[...])
    merged = jnp.zeros(x.shape, F32)
    for i in range(N_BRANCH):
        gate = jax.nn.sigmoid(_dot(h, wg_ref[:, i * D_MODEL:(i + 1) * D_MODEL]))
        merged = merged + gate * _dot(ys[i], p_ref[i])
    o_ref[...] = x + mod_ref[2:3, :] * _dot(merged.astype(BF16), wo_ref[...])


def _merge_call(x, mod, gmix, wg, ya, yb, hf, hb, so, hng, yd, p, wo, *, tm):
    bsz, seq, _ = x.shape
    tok = lambda width: pl.BlockSpec((None, tm, width), lambda b, i: (b, i, 0))
    est = (4 * tm * D_MODEL * 4 + (N_BRANCH + 2) * D_MODEL * D_MODEL * 2 + 2 * tm * BRANCH_W * (3 * 2 + 3 * 4)
           + 8 * tm * D_MODEL * 4 + 8 * 2**20)
    return pl.pallas_call(
        _merge_kernel,
        grid=(bsz, seq // tm),
        in_specs=[
            tok(D_MODEL),
            pl.BlockSpec((None, 8, D_MODEL), lambda b, i: (b, 0, 0)),
            _resident((1, D_MODEL)),
            _resident((D_MODEL, N_BRANCH * D_MODEL)),
            tok(BRANCH_W), tok(BRANCH_W), tok(BRANCH_W), tok(BRANCH_W), tok(BRANCH_W),
            _resident((1, BRANCH_W)),
            tok(BRANCH_W),
            _resident((N_BRANCH, BRANCH_W, D_MODEL)),
            _resident((D_MODEL, D_MODEL)),
        ],
        out_specs=tok(D_MODEL),
        out_shape=jax.ShapeDtypeStruct(x.shape, F32),
        compiler_params=pltpu.CompilerParams(
            dimension_semantics=("parallel", "parallel"), vmem_limit_bytes=_vmem_limit(est)),
        name="merge",
    )(x, mod, gmix, wg, ya, yb, hf, hb, so, hng, yd, p, wo)


def _ffn_kernel(x_ref, mod_ref, gffn_ref, wi_ref, wo_ref, gfin_ref, o_ref, *, fb, final):
    x = x_ref[...]
    h = _mod_norm(x, gffn_ref[...], mod_ref[4:5, :], mod_ref[3:4, :]).astype(BF16)
    acc = jnp.zeros(x.shape, F32)
    for j in range(D_FF // fb):
        gp = _dot(h, wi_ref[:, j * fb:(j + 1) * fb])
        up = _dot(h, wi_ref[:, D_FF + j * fb:D_FF + (j + 1) * fb])
        act = (gp * jax.nn.sigmoid(gp) * up).astype(BF16)
        acc = acc + _dot(act, wo_ref[j * fb:(j + 1) * fb, :])
    y = x + mod_ref[5:6, :] * acc
    if final:
        ms = jnp.mean(y * y, axis=-1, keepdims=True)
        y = y * lax.rsqrt(ms + EPS) * gfin_ref[...]
    o_ref[...] = y


def _ffn_call(x, mod, gffn, wi, wo, gfin, *, tm, fb, final):
    bsz, seq, _ = x.shape
    tok = pl.BlockSpec((None, tm, D_MODEL), lambda b, i: (b, i, 0))
    est = 4 * tm * D_MODEL * 4 + 3 * D_MODEL * D_FF * 2 + 3 * tm * fb * 4 + 4 * tm * D_MODEL * 4 + 8 * 2**20
    return pl.pallas_call(
        functools.partial(_ffn_kernel, fb=fb, final=final),
        grid=(bsz, seq // tm),
        in_specs=[
            tok,
            pl.BlockSpec((None, 8, D_MODEL), lambda b, i: (b, 0, 0)),
            _resident((1, D_MODEL)),
            _resident((D_MODEL, 2 * D_FF)),
            _resident((D_FF, D_MODEL)),
            _resident((1, D_MODEL)),
        ],
        out_specs=tok,
        out_shape=jax.ShapeDtypeStruct(x.shape, F32),
        compiler_params=pltpu.CompilerParams(
            dimension_semantics=("parallel", "parallel"), vmem_limit_bytes=_vmem_limit(est)),
        name="ffn",
    )(x, mod, gffn, wi, wo, gfin)


def _prepare_layer(l, p):
    w_in = p["w_in"][l]
    cols = lambda k: w_in[:, k * BRANCH_W:(k + 1) * BRANCH_W]
    g0 = 9 * BRANCH_W
    gate_cols = jnp.pad(w_in[:, g0:g0 + N_GATES], ((0, 0), (0, GATE_PAD - N_GATES)))
    d0 = g0 + N_GATES
    qk_scale = HEAD_DIM ** -0.5
    w1 = jnp.concatenate(
        [cols(0), cols(1), cols(2) * qk_scale, cols(3), cols(4), cols(5), cols(6) * qk_scale, cols(7), cols(8),
         w_in[:, d0:d0 + BRANCH_W], w_in[:, d0 + BRANCH_W:d0 + 2 * BRANCH_W], gate_cols], axis=1).astype(BF16)
    wg = w_in[:, d0 + 2 * BRANCH_W:].astype(BF16)
    conv_w = jnp.pad(p["d_conv_w"][l], ((0, 32 - CONV_W), (0, 0)))
    row = lambda a: a.reshape(1, -1)
    return dict(
        w1=w1, wg=wg,
        gmix=row(p["g_norm_mix"][l]), gffn=row(p["g_norm_ffn"][l]),
        alng=row(p["a_ln_g"][l]), alnb=row(p["a_ln_b"][l]),
        wsp=p["a_w_sp"][l].reshape(N_HEADS * CHUNK, CHUNK).astype(BF16),
        bsp=jnp.repeat(p["a_b_sp"][l].T, HEAD_DIM, axis=1),
        na_bias=_na_bias_table(p["b_rpb"][l]),
        gate_b=jnp.pad(p["c_gate_b"][l], (0, GATE_PAD - N_GATES)).reshape(1, GATE_PAD),
        hng=row(p["c_hnorm_g"][l]),
        conv_w=conv_w, conv_b=row(p["d_conv_b"][l]), dlng=row(p["d_ln_g"][l]), dlnb=row(p["d_ln_b"][l]),
        wbr=p["w_branch"][l].astype(BF16), wo=p["w_out"][l].astype(BF16),
        wfi=p["w_ffn_in"][l].astype(BF16), wfo=p["w_ffn_out"][l].astype(BF16),
    )


def _trunk(x, mods, layers, g_final, *, tm=512, rb=8, td=512, fb=1408):
    for l, lp in enumerate(layers):
        mod = mods[l]
        (ya, bq, bk, bv, cq, ck, cv, so, cg, dy) = _dense_in_call(
            x, mod, lp["gmix"], lp["w1"], lp["alng"], lp["alnb"], lp["wsp"], lp["bsp"], lp["gate_b"], tm=tm)
        yb = _na_call(bq, bk, bv, lp["na_bias"], rb=rb)
        hf, hb = _mlstm_call(cq, ck, cv, cg)
        yd = _conv_call(dy, lp["conv_w"], lp["conv_b"], lp["dlng"], lp["dlnb"], td=td)
        x = _merge_call(x, mod, lp["gmix"], lp["wg"], ya, yb, hf, hb, so, lp["hng"], yd, lp["wbr"], lp["wo"], tm=tm)
        x = _ffn_call(x, mod, lp["gffn"], lp["wfi"], lp["wfo"], g_final, tm=tm, fb=fb, final=(l == len(layers) - 1))
    return x


def kernel(x_prompt, x_sample, c_prompt, c_sample, w_ada, b_ada, g_norm_mix, g_norm_ffn, w_in, a_ln_g, a_ln_b,
           a_w_sp, a_b_sp, b_rpb, c_gate_b, c_hnorm_g, d_conv_w, d_conv_b, d_ln_g, d_ln_b, w_branch, w_out,
           w_ffn_in, w_ffn_out, g_final):
    p = dict(g_norm_mix=g_norm_mix, g_norm_ffn=g_norm_ffn, w_in=w_in, a_ln_g=a_ln_g, a_ln_b=a_ln_b,
             a_w_sp=a_w_sp, a_b_sp=a_b_sp, b_rpb=b_rpb, c_gate_b=c_gate_b, c_hnorm_g=c_hnorm_g,
             d_conv_w=d_conv_w, d_conv_b=d_conv_b, d_ln_g=d_ln_g, d_ln_b=d_ln_b, w_branch=w_branch,
             w_out=w_out, w_ffn_in=w_ffn_in, w_ffn_out=w_ffn_out)
    layers = [_prepare_layer(l, p) for l in range(DEPTH)]
    nb_p, nb_s = c_prompt.shape[0], c_sample.shape[0]
    n_seq = nb_p + nb_s
    n_pad = -n_seq % 8
    c_all = jnp.pad(jnp.concatenate([c_prompt, c_sample], axis=0), ((0, n_pad), (0, 0)))
    mod = _ada_call(c_all, w_ada, b_ada)
    mod = jnp.pad(mod.reshape(DEPTH, n_seq + n_pad, 6, D_MODEL), ((0, 0), (0, 0), (0, 2), (0, 0)))
    gfin = g_final.reshape(1, D_MODEL)
    y_prompt = _trunk(x_prompt, mod[:, :nb_p], layers, gfin)
    y_sample = _trunk(x_sample, mod[:, nb_p:n_seq], layers, gfin)
    return (y_prompt, y_sample)
```

```python
import functools

import numpy as np
import jax
import jax.numpy as jnp
from jax import lax
from jax.experimental import pallas as pl
from jax.experimental.pallas import tpu as pltpu

F32 = jnp.float32
BF16 = jnp.bfloat16

D_MODEL = 1024
DEPTH = 4
GRID_W = 64
CHUNK = 128
N_BRANCH = 4
BRANCH_W = D_MODEL // N_BRANCH
HEAD_DIM = 64
N_HEADS = BRANCH_W // HEAD_DIM
NA_WIN_R = 8
NA_WIN_C = 16
N_GATES = 4 * N_HEADS
CONV_W = 31
D_FF = 2816
EPS = 1e-6
N_IN = 11 * BRANCH_W + N_GATES + N_BRANCH * D_MODEL

GATE_PAD = 128
N_MAIN = 11 * BRANCH_W + GATE_PAD
NEG_BIG = -1e30
SUBLANES = 8
CONV_HALO = 16

V7X_SCOPED_VMEM_CAP = 60000 * 1024


def _vmem_limit(estimate_bytes):
    return int(min(V7X_SCOPED_VMEM_CAP, estimate_bytes))


def _head_id(shape, axis):
    return lax.shift_right_logical(lax.broadcasted_iota(jnp.int32, shape, axis), 6)


def _mod_norm(x, g, scale, shift):
    ms = jnp.mean(x * x, axis=-1, keepdims=True)
    y = x * lax.rsqrt(ms + EPS) * g
    return y * (1.0 + scale) + shift


def _layer_norm(x, g, b):
    mu = jnp.mean(x, axis=-1, keepdims=True)
    xc = x - mu
    var = jnp.mean(xc * xc, axis=-1, keepdims=True)
    return xc * lax.rsqrt(var + EPS) * g + b


def _dot(a, b):
    return jnp.dot(a, b, preferred_element_type=F32)


def _resident(shape):
    nd = len(shape)
    return pl.BlockSpec(shape, lambda *_: (0,) * nd, pipeline_mode=pl.Buffered(1))


def _ada_kernel(c_ref, w_ref, b_ref, o_ref):
    c = c_ref[...]
    s = (c * jax.nn.sigmoid(c)).astype(BF16)
    o_ref[...] = _dot(s, w_ref[...].astype(BF16)) + b_ref[...]


def _ada_call(c_all, w_ada, b_ada):
    n_seq = c_all.shape[0]
    nb = 1536
    n_out = 6 * D_MODEL
    return pl.pallas_call(
        _ada_kernel,
        grid=(DEPTH, n_out // nb),
        in_specs=[
            pl.BlockSpec((n_seq, D_MODEL), lambda l, j: (0, 0)),
            pl.BlockSpec((None, D_MODEL, nb), lambda l, j: (l, 0, j)),
            pl.BlockSpec((None, 1, nb), lambda l, j: (l, 0, j)),
        ],
        out_specs=pl.BlockSpec((None, n_seq, nb), lambda l, j: (l, 0, j)),
        out_shape=jax.ShapeDtypeStruct((DEPTH, n_seq, n_out), F32),
        compiler_params=pltpu.CompilerParams(
            dimension_semantics=("parallel", "parallel"),
            vmem_limit_bytes=_vmem_limit(32 * 2**20)),
        name="ada_mod",
    )(c_all, w_ada, b_ada.reshape(DEPTH, 1, n_out))


def _dense_in_kernel(x_ref, mod_ref, gmix_ref, w_ref, alng_ref, alnb_ref, wsp_ref, bsp_ref, gb_ref,
                     ya_ref, bq_ref, bk_ref, bv_ref, cq_ref, ck_ref, cv_ref, so_ref, cg_ref, dy_ref,
                     *, tm, sub_tm):
    hid = _head_id((CHUNK, BRANCH_W), 1)
    for st in range(tm // sub_tm):
        rs = slice(st * sub_tm, (st + 1) * sub_tm)
        x = x_ref[rs, :]
        h = _mod_norm(x, gmix_ref[...], mod_ref[1:2, :], mod_ref[0:1, :]).astype(BF16)

        def piece(p, width=BRANCH_W):
            return _dot(h, w_ref[:, p * BRANCH_W:p * BRANCH_W + width])

        gv = _layer_norm(jax.nn.gelu(piece(1)), alng_ref[...], alnb_ref[...]).astype(BF16)
        gu = jax.nn.gelu(piece(0))
        bq_ref[rs, :] = piece(2).astype(BF16)
        bk_ref[rs, :] = piece(3).astype(BF16)
        bv_ref[rs, :] = piece(4).astype(BF16)
        cq_ref[rs, :] = piece(5).astype(BF16)
        ck_ref[rs, :] = piece(6).astype(BF16)
        cv_ref[rs, :] = piece(7).astype(BF16)
        so_ref[rs, :] = jax.nn.sigmoid(piece(8))
        g = piece(11, GATE_PAD) + gb_ref[...]
        lane = lax.broadcasted_iota(jnp.int32, g.shape, 1)
        is_forget = (lane < N_GATES) & ((lane & 4) == 4)
        log_sig = jnp.minimum(g, 0.0) - jnp.log1p(jnp.exp(-jnp.abs(g)))
        cg_ref[rs, :] = jnp.where(is_forget, log_sig, g)
        dy_ref[rs, :] = piece(9) * jax.nn.sigmoid(piece(10))
        for c in range(sub_tm // CHUNK):
            rows = slice(c * CHUNK, (c + 1) * CHUNK)
            s_all = _dot(wsp_ref[...], gv[rows, :])
            s = jnp.zeros((CHUNK, BRANCH_W), F32)
            for g in range(N_HEADS):
                s = jnp.where(hid == g, s_all[g * CHUNK:(g + 1) * CHUNK, :], s)
            out_rows = slice(st * sub_tm + c * CHUNK, st * sub_tm + (c + 1) * CHUNK)
            ya_ref[out_rows, :] = (gu[rows, :] * (s + bsp_ref[...])).astype(BF16)


def _dense_in_call(x, mod, gmix, w1, alng, alnb, wsp, bsp, gb, *, tm, sub_tm):
    bsz, seq, _ = x.shape
    assert tm % sub_tm == 0 and sub_tm % CHUNK == 0
    tok = lambda width: pl.BlockSpec((None, tm, width), lambda b, i: (b, i, 0))
    out_bf = jax.ShapeDtypeStruct((bsz, seq, BRANCH_W), BF16)
    out_f = jax.ShapeDtypeStruct((bsz, seq, BRANCH_W), F32)
    est = (2 * tm * D_MODEL * 4 + D_MODEL * N_MAIN * 2 + 2 * tm * (7 * BRANCH_W * 2 + 2 * BRANCH_W * 4 + GATE_PAD * 4)
           + 8 * tm * D_MODEL * 4 + 8 * 2**20)
    return pl.pallas_call(
        functools.partial(_dense_in_kernel, tm=tm, sub_tm=sub_tm),
        grid=(bsz, seq // tm),
        in_specs=[
            tok(D_MODEL),
            pl.BlockSpec((None, 8, D_MODEL), lambda b, i: (b, 0, 0)),
            _resident((1, D_MODEL)),
            _resident((D_MODEL, N_MAIN)),
            _resident((1, BRANCH_W)),
            _resident((1, BRANCH_W)),
            _resident((N_HEADS * CHUNK, CHUNK)),
            _resident((CHUNK, BRANCH_W)),
            _resident((1, GATE_PAD)),
        ],
        out_specs=[tok(BRANCH_W)] * 8 + [tok(GATE_PAD), tok(BRANCH_W)],
        out_shape=[out_bf] * 7 + [out_f, jax.ShapeDtypeStruct((bsz, seq, GATE_PAD), F32), out_f],
        compiler_params=pltpu.CompilerParams(
            dimension_semantics=("parallel", "parallel"), vmem_limit_bytes=_vmem_limit(est)),
        name="dense_in",
    )(x, mod, gmix, w1, alng, alnb, wsp, bsp, gb)


def _na_kernel(q_ref, k_ref, v_ref, bias_ref, o_ref, *, rows, rb, unroll):
    i = pl.program_id(1)
    win = NA_WIN_R * GRID_W
    hid = _head_id((GRID_W, BRANCH_W), 1)

    def body(jj, carry):
        qoffs, koffs, scores = [], [], []
        for u in range(unroll):
            j = jj * unroll + u
            r = i * rb + j
            start = jnp.clip(r - NA_WIN_R // 2, 0, rows - NA_WIN_R)
            delta = r - start
            qoff = pl.multiple_of(j * GRID_W, GRID_W)
            q = q_ref[pl.ds(qoff, GRID_W), :].astype(F32)
            qs = jnp.concatenate([jnp.where(hid == h, q, 0.0) for h in range(N_HEADS)], axis=0).astype(BF16)
            koff = pl.multiple_of(start * GRID_W, GRID_W)
            kw = k_ref[pl.ds(koff, win), :]
            s = lax.dot_general(qs, kw, (((1,), (1,)), ((), ())), preferred_element_type=F32)
            scores.append(s + bias_ref[delta])
            qoffs.append(qoff)
            koffs.append(koff)
        results = []
        for u in range(unroll):
            s = scores[u]
            m = jnp.max(s, axis=-1, keepdims=True)
            e = jnp.exp(s - m)
            inv = 1.0 / jnp.sum(e, axis=-1, keepdims=True)
            results.append((_dot(e.astype(BF16), v_ref[pl.ds(koffs[u], win), :]), inv))
        for u in range(unroll):
            res, inv = results[u]
            res = res * inv
            o = jnp.zeros((GRID_W, BRANCH_W), F32)
            for h in range(N_HEADS):
                o = jnp.where(hid == h, res[h * GRID_W:(h + 1) * GRID_W, :], o)
            o_ref[pl.ds(qoffs[u], GRID_W), :] = o.astype(BF16)
        return carry

    lax.fori_loop(0, rb // unroll, body, 0)


def _na_call(q, k, v, bias, *, rb, unroll=2):
    bsz, seq, _ = q.shape
    rows = seq // GRID_W
    assert rows >= NA_WIN_R and rows % rb == 0 and rb % unroll == 0
    tq = rb * GRID_W
    win = NA_WIN_R * GRID_W
    est = 4 * tq * BRANCH_W * 2 + 4 * seq * BRANCH_W * 2 + bias.size * 4 + 16 * 2**20
    return pl.pallas_call(
        functools.partial(_na_kernel, rows=rows, rb=rb, unroll=unroll),
        grid=(bsz, rows // rb),
        in_specs=[
            pl.BlockSpec((None, tq, BRANCH_W), lambda b, i: (b, i, 0)),
            pl.BlockSpec((None, seq, BRANCH_W), lambda b, i: (b, 0, 0)),
            pl.BlockSpec((None, seq, BRANCH_W), lambda b, i: (b, 0, 0)),
            _resident((NA_WIN_R, N_HEADS * GRID_W, win)),
        ],
        out_specs=pl.BlockSpec((None, tq, BRANCH_W), lambda b, i: (b, i, 0)),
        out_shape=jax.ShapeDtypeStruct((bsz, seq, BRANCH_W), BF16),
        compiler_params=pltpu.CompilerParams(
            dimension_semantics=("parallel", "parallel"), vmem_limit_bytes=_vmem_limit(est)),
        name="nbr_attn",
    )(q, k, v, bias)


def _na_bias_table(rpb):
    qc = np.arange(GRID_W)[:, None]
    kc = np.arange(GRID_W)[None, :]
    dc = kc - qc + NA_WIN_C - 1
    onehot = (dc[None] == np.arange(2 * NA_WIN_C - 1)[:, None, None]).astype(np.float32)
    c0 = np.clip(qc - NA_WIN_C // 2, 0, GRID_W - NA_WIN_C)
    valid = (kc >= c0) & (kc < c0 + NA_WIN_C)
    toe = jnp.einsum("hrd,dqk->hrqk", rpb.astype(F32), onehot, precision=lax.Precision.HIGHEST)
    toe = jnp.where(valid[None, None], toe, NEG_BIG)
    tab = jnp.stack([toe[:, NA_WIN_R - 1 - dl:2 * NA_WIN_R - 1 - dl] for dl in range(NA_WIN_R)], axis=0)
    tab = tab.transpose(0, 1, 3, 2, 4)
    return tab.reshape(NA_WIN_R, N_HEADS * GRID_W, NA_WIN_R * GRID_W)


def _split3(x):
    hi = x.astype(BF16)
    r1 = x - hi.astype(F32)
    mid = r1.astype(BF16)
    lo = (r1 - mid.astype(F32)).astype(BF16)
    return hi, mid, lo


class _Chain:
    pass


HEAD_ROW0 = N_HEADS


def _mlstm_stage_mxu_in(ch):
    ch.gates = ch.g_ref[ch.sq]
    g_hi, g_mid, g_lo = _split3(ch.gates)
    ch.b_all = _dot(ch.tri, g_hi) + _dot(ch.tri, g_mid) + _dot(ch.tri, g_lo)
    lo_row = lax.broadcasted_iota(jnp.int32, (1, CHUNK), 1) < HEAD_DIM
    keep_lo = jnp.where(lo_row, 1.0, 0.0).astype(BF16)
    keep_hi = jnp.where(lo_row, 0.0, 1.0).astype(BF16)
    row16 = lax.broadcasted_iota(jnp.int32, (2 * SUBLANES, CHUNK), 0)
    lo16 = lax.broadcasted_iota(jnp.int32, (2 * SUBLANES, CHUNK), 1) < HEAD_DIM
    ch.k, ch.m1, ch.ct_prev, ch.n_prev = [], [], [], []
    for p in range(N_HEADS // 2):
        lanes = slice(p * CHUNK, (p + 1) * CHUNK)
        q_p = ch.q_ref[ch.sq, :, lanes]
        k_p = ch.k_ref[ch.sq, :, lanes]
        ct_prev = ch.ct_ref[p]
        n_prev = ch.n_ref[:, lanes]
        r0 = HEAD_ROW0 + 2 * p
        n_rows = jnp.where((row16 == r0) & lo16, n_prev,
                           jnp.where((row16 == r0 + 1) & jnp.logical_not(lo16), n_prev, 0.0)).astype(BF16)
        lhs = jnp.concatenate([k_p * keep_lo, k_p * keep_hi, ct_prev.astype(BF16), n_rows], axis=0)
        ch.m1.append(lax.dot_general(lhs, q_p, (((1,), (1,)), ((), ())), preferred_element_type=F32))
        ch.k.append(k_p)
        ch.ct_prev.append(ct_prev)
        ch.n_prev.append(n_prev)


def _mlstm_stage_state(ch):
    d = ch.d
    lane_a = lax.broadcasted_iota(jnp.int32, (CHUNK, CHUNK), 1)
    sub_a = lax.broadcasted_iota(jnp.int32, (CHUNK, CHUNK), 0)
    same_head = (lane_a < HEAD_DIM) == (sub_a < HEAD_DIM)
    lo_row = lax.broadcasted_iota(jnp.int32, (1, CHUNK), 1) < HEAD_DIM
    lane8 = lax.broadcasted_iota(jnp.int32, (SUBLANES, CHUNK), 1)

    b_all = ch.b_all
    ch.r_a = pltpu.roll(ch.gates, N_HEADS, axis=1) - b_all
    x_t = jnp.where((lane_a & N_HEADS) == N_HEADS, b_all, ch.gates).T
    pd = x_t[SUBLANES * d:SUBLANES * (d + 1), :]
    li_b = pltpu.roll(pd, N_HEADS, axis=0)
    r_b = li_b - pd
    cm = r_b
    sh = 1
    while sh < CHUNK:
        if ch.last == 0:
            cm = jnp.maximum(cm, jnp.where(lane8 < CHUNK - sh, pltpu.roll(cm, CHUNK - sh, axis=1), NEG_BIG))
        else:
            cm = jnp.maximum(cm, jnp.where(lane8 >= sh, pltpu.roll(cm, sh, axis=1), NEG_BIG))
        sh *= 2
    m_prev = ch.m_ref[...]
    m_inter = pd + m_prev
    m_t = jnp.maximum(m_inter, pd + cm)
    ch.sc = jnp.exp(m_inter - m_t)
    ch.u = pd - m_t
    ch.em = jnp.exp(-m_t)
    b_last = jnp.broadcast_to(pd[:, ch.last:ch.last + 1], pd.shape)
    a = b_last - pd + li_b
    m_loc = jnp.max(a, axis=1, keepdims=True)
    w = jnp.exp(a - m_loc)
    m_new = jnp.maximum(b_last + m_prev, m_loc)
    f_old = jnp.exp(b_last + m_prev - m_new)
    f_loc = jnp.exp(m_loc - m_new)
    ch.m_ref[...] = m_new

    w_rows = jnp.concatenate([w, jnp.zeros_like(w)], axis=0).astype(BF16)
    ch.vt = []
    for p in range(N_HEADS // 2):
        lanes = slice(p * CHUNK, (p + 1) * CHUNK)
        r0 = HEAD_ROW0 + 2 * p
        v_t = ch.v_ref[ch.sq, :, lanes].astype(F32).T
        w_sel = jnp.where(sub_a < HEAD_DIM, w[r0:r0 + 1, :], w[r0 + 1:r0 + 2, :])
        lhs = jnp.concatenate([(v_t * w_sel).astype(BF16), w_rows], axis=0)
        m3 = _dot(lhs, ch.k[p])
        s_n = jnp.where(lo_row, m3[CHUNK + r0:CHUNK + r0 + 1, :], m3[CHUNK + r0 + 1:CHUNK + r0 + 2, :])
        fo = jnp.where(lo_row, f_old[r0:r0 + 1, :], f_old[r0 + 1:r0 + 2, :])
        fl = jnp.where(lo_row, f_loc[r0:r0 + 1, :], f_loc[r0 + 1:r0 + 2, :])
        ch.ct_ref[p] = fo * ch.ct_prev[p] + fl * jnp.where(same_head, m3[:CHUNK, :], 0.0)
        ch.n_ref[:, lanes] = fo * ch.n_prev[p] + fl * s_n
        ch.vt.append(v_t.astype(BF16))


def _mlstm_stage_intra(ch):
    lf0 = SUBLANES * ch.d + N_HEADS
    ones = jnp.ones((2 * SUBLANES, CHUNK), BF16)
    ch.res_t, ch.row_sum = [], []
    for h in range(N_HEADS):
        p, hl = divmod(h, 2)
        col, row = lf0 + h, HEAD_ROW0 + h
        d_log = jnp.where(ch.causal_t, ch.r_a[:, col:col + 1] + ch.u[row:row + 1, :], NEG_BIG)
        w_t = (ch.m1[p][hl * CHUNK:(hl + 1) * CHUNK, :] * jnp.exp(d_log)).astype(BF16)
        lhs = jnp.concatenate([ch.vt[p][hl * HEAD_DIM:(hl + 1) * HEAD_DIM, :], ones], axis=0)
        m2 = _dot(lhs, w_t)
        ch.res_t.append(m2[:HEAD_DIM, :])
        ch.row_sum.append(m2[HEAD_DIM:HEAD_DIM + 1, :])


def _mlstm_stage_out(ch):
    row8 = lax.broadcasted_iota(jnp.int32, (SUBLANES, CHUNK), 0)
    qn_at = 2 * CHUNK + CHUNK
    qn = jnp.where(row8 < HEAD_ROW0 + 2, ch.m1[0][qn_at:qn_at + SUBLANES, :], ch.m1[1][qn_at:qn_at + SUBLANES, :])
    den = jnp.zeros((SUBLANES, CHUNK), F32)
    for h in range(N_HEADS):
        den = jnp.where(row8 == HEAD_ROW0 + h, ch.row_sum[h], den)
    den = den + ch.sc * qn
    f_intra = 1.0 / jnp.maximum(jnp.abs(den), ch.em)
    f_inter = ch.sc * f_intra
    for p in range(N_HEADS // 2):
        qc_t = ch.m1[p][2 * CHUNK:3 * CHUNK, :]
        halves = []
        for hl in range(2):
            row = HEAD_ROW0 + 2 * p + hl
            halves.append(ch.res_t[2 * p + hl] * f_intra[row:row + 1, :]
                          + qc_t[hl * HEAD_DIM:(hl + 1) * HEAD_DIM, :] * f_inter[row:row + 1, :])
        ch.h_ref[ch.sq, :, p * CHUNK:(p + 1) * CHUNK] = jnp.concatenate(halves, axis=0).T


def _mlstm_kernel(qf_ref, kf_ref, vf_ref, gf_ref, qb_ref, kb_ref, vb_ref, gb_ref,
                  hf_ref, hb_ref, ct_ref, n_ref, m_ref, *, sb):
    @pl.when(pl.program_id(1) == 0)
    def _():
        for ref in (ct_ref, n_ref, m_ref):
            ref[...] = jnp.zeros(ref.shape, F32)

    t = lax.broadcasted_iota(jnp.int32, (CHUNK, CHUNK), 0)
    s = lax.broadcasted_iota(jnp.int32, (CHUNK, CHUNK), 1)
    lower = s <= t
    upper = s >= t
    tri_f = jnp.where(lower, 1.0, 0.0).astype(BF16)
    tri_b = jnp.where(upper, 1.0, 0.0).astype(BF16)
    chains = []
    for sq in range(sb):
        for d, refs, tri, causal_t, last, h_ref in (
                (0, (qf_ref, kf_ref, vf_ref, gf_ref), tri_f, upper, CHUNK - 1, hf_ref),
                (1, (qb_ref, kb_ref, vb_ref, gb_ref), tri_b, lower, 0, hb_ref)):
            ch = _Chain()
            ch.q_ref, ch.k_ref, ch.v_ref, ch.g_ref = refs
            ch.sq, ch.d, ch.tri, ch.causal_t, ch.last, ch.h_ref = sq, d, tri, causal_t, last, h_ref
            ch.ct_ref, ch.n_ref, ch.m_ref = ct_ref.at[d, sq], n_ref.at[d, sq], m_ref.at[d, sq]
            chains.append(ch)
    for stage in (_mlstm_stage_mxu_in, _mlstm_stage_state, _mlstm_stage_intra, _mlstm_stage_out):
        for ch in chains:
            stage(ch)


def _mlstm_call(q, k, v, gates, *, sb):
    bsz, seq, _ = q.shape
    assert bsz % sb == 0
    n = seq // CHUNK
    fwd = lambda width: pl.BlockSpec((sb, CHUNK, width), lambda b, i: (b, i, 0))
    bwd = lambda width: pl.BlockSpec((sb, CHUNK, width), lambda b, i: (b, n - 1 - i, 0))
    out = jax.ShapeDtypeStruct((bsz, seq, BRANCH_W), F32)
    return pl.pallas_call(
        functools.partial(_mlstm_kernel, sb=sb),
        grid=(bsz // sb, n),
        in_specs=[fwd(BRANCH_W)] * 3 + [fwd(GATE_PAD)] + [bwd(BRANCH_W)] * 3 + [bwd(GATE_PAD)],
        out_specs=[fwd(BRANCH_W), bwd(BRANCH_W)],
        out_shape=[out, out],
        scratch_shapes=[pltpu.VMEM((2, sb, N_HEADS // 2, CHUNK, CHUNK), F32),
                        pltpu.VMEM((2, sb, 1, BRANCH_W), F32),
                        pltpu.VMEM((2, sb, SUBLANES, CHUNK), F32)],
        compiler_params=pltpu.CompilerParams(
            dimension_semantics=("arbitrary", "arbitrary"), vmem_limit_bytes=_vmem_limit(32 * 2**20)),
        name="mlstm",
    )(q, k, v, gates, q, k, v, gates)


def _conv_kernel(yc_ref, yp_ref, yn_ref, w_ref, cb_ref, g_ref, b_ref, o_ref, buf_ref, sh_ref, *, td, sub):
    i = pl.program_id(1)
    nt = pl.num_programs(1)
    buf_ref[0:CONV_HALO, :] = jnp.where(i > 0, yp_ref[...], 0.0)
    buf_ref[CONV_HALO:CONV_HALO + td, :] = yc_ref[...]
    buf_ref[CONV_HALO + td:2 * CONV_HALO + td, :] = jnp.where(i < nt - 1, yn_ref[...], 0.0)
    n_sh = td + 2 * CONV_HALO - SUBLANES
    for r in range(SUBLANES):
        sh_ref[r] = buf_ref[r:r + n_sh, :]
    first = CONV_HALO - CONV_W // 2

    for sblk in range(td // sub):
        base = sblk * sub
        acc = jnp.zeros((sub, BRANCH_W), F32)
        for j in range(CONV_W):
            off = first + j
            lo = base + off // SUBLANES * SUBLANES
            acc = acc + w_ref[j:j + 1, :] * sh_ref[off % SUBLANES, lo:lo + sub, :]
        y = _layer_norm(acc + cb_ref[...], g_ref[...], b_ref[...])
        o_ref[base:base + sub, :] = (y * jax.nn.sigmoid(y)).astype(BF16)


def _conv_call(y, w, cb, g, b, *, td, sub=64):
    bsz, seq, _ = y.shape
    hb = td // CONV_HALO
    n_halo = seq // CONV_HALO
    return pl.pallas_call(
        functools.partial(_conv_kernel, td=td, sub=sub),
        grid=(bsz, seq // td),
        in_specs=[
            pl.BlockSpec((None, td, BRANCH_W), lambda b_, i: (b_, i, 0)),
            pl.BlockSpec((None, CONV_HALO, BRANCH_W), lambda b_, i: (b_, jnp.maximum(i * hb - 1, 0), 0)),
            pl.BlockSpec((None, CONV_HALO, BRANCH_W), lambda b_, i: (b_, jnp.minimum((i + 1) * hb, n_halo - 1), 0)),
            _resident((32, BRANCH_W)),
            _resident((1, BRANCH_W)),
            _resident((1, BRANCH_W)),
            _resident((1, BRANCH_W)),
        ],
        out_specs=pl.BlockSpec((None, td, BRANCH_W), lambda b_, i: (b_, i, 0)),
        out_shape=jax.ShapeDtypeStruct((bsz, seq, BRANCH_W), BF16),
        scratch_shapes=[pltpu.VMEM((td + 2 * CONV_HALO, BRANCH_W), F32),
                        pltpu.VMEM((SUBLANES, td + 2 * CONV_HALO - SUBLANES, BRANCH_W), F32)],
        compiler_params=pltpu.CompilerParams(
            dimension_semantics=("parallel", "parallel"), vmem_limit_bytes=_vmem_limit(32 * 2**20)),
        name="conv_module",
    )(y, y, y, w, cb, g, b)


def _merge_kernel(x_ref, mod_ref, gmix_ref, wg_ref, ya_ref, yb_ref, hf_ref, hb_ref, so_ref, hng_ref, yd_ref,
                  p_ref, wo_ref, o_ref, *, tm, sub_tm):
    xs, merged = [], []
    for st in range(tm // sub_tm):
        rs = slice(st * sub_tm, (st + 1) * sub_tm)
        x = x_ref[rs, :]
        h = _mod_norm(x, gmix_ref[...], mod_ref[1:2, :], mod_ref[0:1, :]).astype(BF16)

        hh = hf_ref[rs, :] + hb_ref[rs, :]
        hid = _head_id(hh.shape, 1)
        sq = hh * hh
        rinv = jnp.zeros(hh.shape, F32)
        for hd in range(N_HEADS):
            ms = jnp.sum(jnp.where(hid == hd, sq, 0.0), axis=-1, keepdims=True) * (1.0 / HEAD_DIM)
            rinv = jnp.where(hid == hd, lax.rsqrt(ms + EPS), rinv)
        yc = (so_ref[rs, :] * (hh * rinv * hng_ref[...])).astype(BF16)

        ys = (ya_ref[rs, :], yb_ref[rs, :], yc, yd_ref[rs, :])
        m = jnp.zeros(x.shape, F32)
        for i in range(N_BRANCH):
            gate = jax.nn.sigmoid(_dot(h, wg_ref[:, i * D_MODEL:(i + 1) * D_MODEL]))
            m = m + gate * _dot(ys[i], p_ref[i])
        xs.append(x)
        merged.append(m.astype(BF16))
    for st in range(tm // sub_tm):
        rs = slice(st * sub_tm, (st + 1) * sub_tm)
        o_ref[rs, :] = xs[st] + mod_ref[2:3, :] * _dot(merged[st], wo_ref[...])


def _merge_call(x, mod, gmix, wg, ya, yb, hf, hb, so, hng, yd, p, wo, *, tm, sub_tm):
    bsz, seq, _ = x.shape
    assert tm % sub_tm == 0
    tok = lambda width: pl.BlockSpec((None, tm, width), lambda b, i: (b, i, 0))
    est = (4 * tm * D_MODEL * 4 + (N_BRANCH + 2) * D_MODEL * D_MODEL * 2 + 2 * tm * BRANCH_W * (3 * 2 + 3 * 4)
           + 8 * tm * D_MODEL * 4 + 8 * 2**20)
    return pl.pallas_call(
        functools.partial(_merge_kernel, tm=tm, sub_tm=sub_tm),
        grid=(bsz, seq // tm),
        in_specs=[
            tok(D_MODEL),
            pl.BlockSpec((None, 8, D_MODEL), lambda b, i: (b, 0, 0)),
            _resident((1, D_MODEL)),
            _resident((D_MODEL, N_BRANCH * D_MODEL)),
            tok(BRANCH_W), tok(BRANCH_W), tok(BRANCH_W), tok(BRANCH_W), tok(BRANCH_W),
            _resident((1, BRANCH_W)),
            tok(BRANCH_W),
            _resident((N_BRANCH, BRANCH_W, D_MODEL)),
            _resident((D_MODEL, D_MODEL)),
        ],
        out_specs=tok(D_MODEL),
        out_shape=jax.ShapeDtypeStruct(x.shape, F32),
        compiler_params=pltpu.CompilerParams(
            dimension_semantics=("parallel", "parallel"), vmem_limit_bytes=_vmem_limit(est)),
        name="merge",
    )(x, mod, gmix, wg, ya, yb, hf, hb, so, hng, yd, p, wo)


def _ffn_kernel(x_ref, mod_ref, gffn_ref, wi_ref, wo_ref, gfin_ref, o_ref, *, tm, sub_tm, fb, final):
    n_fb = D_FF // fb
    xs, acts = [], []
    for st in range(tm // sub_tm):
        x = x_ref[st * sub_tm:(st + 1) * sub_tm, :]
        h = _mod_norm(x, gffn_ref[...], mod_ref[4:5, :], mod_ref[3:4, :]).astype(BF16)
        a = []
        for j in range(n_fb):
            gp = _dot(h, wi_ref[:, j * fb:(j + 1) * fb])
            up = _dot(h, wi_ref[:, D_FF + j * fb:D_FF + (j + 1) * fb])
            a.append((gp * jax.nn.sigmoid(gp) * up).astype(BF16))
        xs.append(x)
        acts.append(a)
    for st in range(tm // sub_tm):
        acc = _dot(acts[st][0], wo_ref[0:fb, :])
        for j in range(1, n_fb):
            acc = acc + _dot(acts[st][j], wo_ref[j * fb:(j + 1) * fb, :])
        y = xs[st] + mod_ref[5:6, :] * acc
        if final:
            ms = jnp.mean(y * y, axis=-1, keepdims=True)
            y = y * lax.rsqrt(ms + EPS) * gfin_ref[...]
        o_ref[st * sub_tm:(st + 1) * sub_tm, :] = y


def _ffn_call(x, mod, gffn, wi, wo, gfin, *, tm, sub_tm, fb, final):
    bsz, seq, _ = x.shape
    assert tm % sub_tm == 0 and D_FF % fb == 0
    tok = pl.BlockSpec((None, tm, D_MODEL), lambda b, i: (b, i, 0))
    est = 4 * tm * D_MODEL * 4 + 3 * D_MODEL * D_FF * 2 + 3 * tm * fb * 4 + 4 * tm * D_MODEL * 4 + 8 * 2**20
    return pl.pallas_call(
        functools.partial(_ffn_kernel, tm=tm, sub_tm=sub_tm, fb=fb, final=final),
        grid=(bsz, seq // tm),
        in_specs=[
            tok,
            pl.BlockSpec((None, 8, D_MODEL), lambda b, i: (b, 0, 0)),
            _resident((1, D_MODEL)),
            _resident((D_MODEL, 2 * D_FF)),
            _resident((D_FF, D_MODEL)),
            _resident((1, D_MODEL)),
        ],
        out_specs=tok,
        out_shape=jax.ShapeDtypeStruct(x.shape, F32),
        compiler_params=pltpu.CompilerParams(
            dimension_semantics=("parallel", "parallel"), vmem_limit_bytes=_vmem_limit(est)),
        name="ffn",
    )(x, mod, gffn, wi, wo, gfin)


def _prepare_layer(l, p):
    w_in = p["w_in"][l]
    cols = lambda k: w_in[:, k * BRANCH_W:(k + 1) * BRANCH_W]
    g0 = 9 * BRANCH_W
    gate_cols = jnp.pad(w_in[:, g0:g0 + N_GATES], ((0, 0), (0, GATE_PAD - N_GATES)))
    d0 = g0 + N_GATES
    qk_scale = HEAD_DIM ** -0.5
    w1 = jnp.concatenate(
        [cols(0), cols(1), cols(2) * qk_scale, cols(3), cols(4), cols(5), cols(6) * qk_scale, cols(7), cols(8),
         w_in[:, d0:d0 + BRANCH_W], w_in[:, d0 + BRANCH_W:d0 + 2 * BRANCH_W], gate_cols], axis=1).astype(BF16)
    wg = w_in[:, d0 + 2 * BRANCH_W:].astype(BF16)
    conv_w = jnp.pad(p["d_conv_w"][l], ((0, 32 - CONV_W), (0, 0)))
    row = lambda a: a.reshape(1, -1)
    return dict(
        w1=w1, wg=wg,
        gmix=row(p["g_norm_mix"][l]), gffn=row(p["g_norm_ffn"][l]),
        alng=row(p["a_ln_g"][l]), alnb=row(p["a_ln_b"][l]),
        wsp=p["a_w_sp"][l].reshape(N_HEADS * CHUNK, CHUNK).astype(BF16),
        bsp=jnp.repeat(p["a_b_sp"][l].T, HEAD_DIM, axis=1),
        na_bias=_na_bias_table(p["b_rpb"][l]),
        gate_b=jnp.pad(p["c_gate_b"][l], (0, GATE_PAD - N_GATES)).reshape(1, GATE_PAD),
        hng=row(p["c_hnorm_g"][l]),
        conv_w=conv_w, conv_b=row(p["d_conv_b"][l]), dlng=row(p["d_ln_g"][l]), dlnb=row(p["d_ln_b"][l]),
        wbr=p["w_branch"][l].astype(BF16), wo=p["w_out"][l].astype(BF16),
        wfi=p["w_ffn_in"][l].astype(BF16), wfo=p["w_ffn_out"][l].astype(BF16),
    )


def _trunk(x, mods, layers, g_final, *, tm=512, sub_tm=256, rb=8, td=512, fb=1408):
    sb = max(s for s in (4, 2, 1) if x.shape[0] % s == 0)
    for l, lp in enumerate(layers):
        mod = mods[l]
        (ya, bq, bk, bv, cq, ck, cv, so, cg, dy) = _dense_in_call(
            x, mod, lp["gmix"], lp["w1"], lp["alng"], lp["alnb"], lp["wsp"], lp["bsp"], lp["gate_b"],
            tm=tm, sub_tm=sub_tm)
        yb = _na_call(bq, bk, bv, lp["na_bias"], rb=rb)
        hf, hb = _mlstm_call(cq, ck, cv, cg, sb=sb)
        yd = _conv_call(dy, lp["conv_w"], lp["conv_b"], lp["dlng"], lp["dlnb"], td=td)
        x = _merge_call(x, mod, lp["gmix"], lp["wg"], ya, yb, hf, hb, so, lp["hng"], yd, lp["wbr"], lp["wo"],
                        tm=tm, sub_tm=sub_tm)
        x = _ffn_call(x, mod, lp["gffn"], lp["wfi"], lp["wfo"], g_final, tm=tm, sub_tm=sub_tm, fb=fb,
                      final=(l == len(layers) - 1))
    return x


def kernel(x_prompt, x_sample, c_prompt, c_sample, w_ada, b_ada, g_norm_mix, g_norm_ffn, w_in, a_ln_g, a_ln_b,
           a_w_sp, a_b_sp, b_rpb, c_gate_b, c_hnorm_g, d_conv_w, d_conv_b, d_ln_g, d_ln_b, w_branch, w_out,
           w_ffn_in, w_ffn_out, g_final):
    p = dict(g_norm_mix=g_norm_mix, g_norm_ffn=g_norm_ffn, w_in=w_in, a_ln_g=a_ln_g, a_ln_b=a_ln_b,
             a_w_sp=a_w_sp, a_b_sp=a_b_sp, b_rpb=b_rpb, c_gate_b=c_gate_b, c_hnorm_g=c_hnorm_g,
             d_conv_w=d_conv_w, d_conv_b=d_conv_b, d_ln_g=d_ln_g, d_ln_b=d_ln_b, w_branch=w_branch,
             w_out=w_out, w_ffn_in=w_ffn_in, w_ffn_out=w_ffn_out)
    layers = [_prepare_layer(l, p) for l in range(DEPTH)]
    nb_p, nb_s = c_prompt.shape[0], c_sample.shape[0]
    n_seq = nb_p + nb_s
    n_pad = -n_seq % 8
    c_all = jnp.pad(jnp.concatenate([c_prompt, c_sample], axis=0), ((0, n_pad), (0, 0)))
    mod = _ada_call(c_all, w_ada, b_ada)
    mod = jnp.pad(mod.reshape(DEPTH, n_seq + n_pad, 6, D_MODEL), ((0, 0), (0, 0), (0, 2), (0, 0)))
    gfin = g_final.reshape(1, D_MODEL)
    y_prompt = _trunk(x_prompt, mod[:, :nb_p], layers, gfin)
    y_sample = _trunk(x_sample, mod[:, nb_p:n_seq], layers, gfin)
    return (y_prompt, y_sample)
```
